```python
import math
import jax, jax.numpy as jnp
from jax import lax
import numpy as np

D_MODEL = 1024
BATCH = 2
SEQ = 16384
DEPTH = 1

CHUNK = 64
Q_BLOCK = 128
DA_HEADS = 4
DA_QK_DIM = 64
DA_V_DIM = 2 * DA_QK_DIM
DA_WIDTH = DA_HEADS * DA_V_DIM
ML_HEADS = 4
ML_HEAD_DIM = 128
ML_WIDTH = ML_HEADS * ML_HEAD_DIM
CONV_K = 4
MIX_WIDTH = DA_WIDTH + ML_WIDTH
IN_GROUPS = (2 * DA_HEADS * DA_QK_DIM,
             2 * DA_HEADS * DA_QK_DIM,
             DA_WIDTH,
             ML_WIDTH,
             ML_WIDTH,
             ML_WIDTH,
             ML_WIDTH,
             ML_HEADS,
             ML_HEADS)
IN_COLS = 3 * 512 + 4 * 512 + 2 * ML_HEADS
D_FF = 2816
N_BUCKETS = 32
MAX_DISTANCE = 128
LN_EPS = 1e-5
ALPHA = (2.0 * DEPTH) ** 0.25
BETA = (8.0 * DEPTH) ** -0.25

kernel_name = "hybrid_diffattn_mlstm_macaron_deepnorm"


def layer_norm(x, g, b):
    xf = x.astype(jnp.float32)
    mu = jnp.mean(xf, -1, keepdims=True)
    var = jnp.mean(jnp.square(xf - mu), -1, keepdims=True)
    return ((xf - mu) * lax.rsqrt(var + LN_EPS)).astype(x.dtype) * g + b


def head_layer_norm(x, g):
    xf = x.astype(jnp.float32)
    mu = jnp.mean(xf, -1, keepdims=True)
    var = jnp.mean(jnp.square(xf - mu), -1, keepdims=True)
    return ((xf - mu) * lax.rsqrt(var + LN_EPS)).astype(x.dtype) * g


def rms_norm(x, g):
    xf = x.astype(jnp.float32)
    return (xf * lax.rsqrt(jnp.mean(jnp.square(xf), -1, keepdims=True) + LN_EPS)).astype(x.dtype) * g


def swiglu(x, w_up, w_down):
    a, u = jnp.split(x @ w_up, 2, axis=-1)
    return (jax.nn.silu(a) * u) @ w_down


def t5_bucket(rel):
    nb = N_BUCKETS // 2
    max_exact = nb // 2
    ret = jnp.where(rel > 0, nb, 0)
    n = jnp.abs(rel)
    nf = jnp.maximum(n, max_exact).astype(jnp.float32)
    large = max_exact + (jnp.log(nf / max_exact) / math.log(MAX_DISTANCE / max_exact)
                         * (nb - max_exact)).astype(jnp.int32)
    large = jnp.minimum(large, nb - 1)
    return ret + jnp.where(n < max_exact, n, large)


def diff_attention(q1, q2, k1, k2, v, lam, rel_bias):
    B, H, S, dq = q1.shape
    scale = dq ** -0.5
    k_pos = jnp.arange(S)

    def one_block(i):
        start = i * Q_BLOCK
        qb1 = lax.dynamic_slice_in_dim(q1, start, Q_BLOCK, axis=2)
        qb2 = lax.dynamic_slice_in_dim(q2, start, Q_BLOCK, axis=2)
        q_pos = start + jnp.arange(Q_BLOCK)
        bias = jnp.transpose(rel_bias[t5_bucket(k_pos[None, :] - q_pos[:, None])], (2, 0, 1))
        allowed = (k_pos[None, :] // CHUNK) <= (q_pos[:, None] // CHUNK)

        def probs(qb, k):
            s = jnp.einsum('bhqd,bhkd->bhqk', qb, k).astype(jnp.float32) * scale + bias
            return jax.nn.softmax(jnp.where(allowed, s, -jnp.inf), axis=-1)

        a = probs(qb1, k1) - lam * probs(qb2, k2)
        return jnp.einsum('bhqk,bhkd->bhqd', a.astype(v.dtype), v)

    out = lax.map(one_block, jnp.arange(S // Q_BLOCK))
    return jnp.transpose(out, (1, 2, 0, 3, 4)).reshape(B, H, S, v.shape[-1])


def mlstm_chunkwise(q, k, v, i_pre, f_pre):
    B, H, S, D = q.shape
    NC, L = S // CHUNK, CHUNK
    f32 = jnp.float32
    qc = q.astype(f32).reshape(B, H, NC, L, D)
    kc = (k.astype(f32) * D ** -0.5).reshape(B, H, NC, L, D)
    vc = v.astype(f32).reshape(B, H, NC, L, D)
    li = i_pre.astype(f32).reshape(B, H, NC, L)
    lf = jax.nn.log_sigmoid(f_pre.astype(f32)).reshape(B, H, NC, L)
    b = jnp.cumsum(lf, axis=-1)
    b_last = b[..., -1]
    g = b_last[..., None] - b + li
    m_loc = jnp.max(g, -1)
    w = jnp.exp(g - m_loc[..., None])
    kv_loc = jnp.einsum('bhcl,bhclv,bhclk->bhcvk', w, vc, kc)
    n_loc = jnp.einsum('bhcl,bhclk->bhck', w, kc)

    def step(carry, xs):
        C, n, m = carry
        bl, ml, kvl, nl = xs
        m_new = jnp.maximum(bl + m, ml)
        decay = jnp.exp(bl + m - m_new)
        sc = jnp.exp(ml - m_new)
        C_new = decay[..., None, None] * C + sc[..., None, None] * kvl
        n_new = decay[..., None] * n + sc[..., None] * nl
        return (C_new, n_new, m_new), (C, n, m)

    init = (jnp.zeros((B, H, D, D), f32), jnp.zeros((B, H, D), f32), jnp.zeros((B, H), f32))
    xs = (jnp.moveaxis(b_last, 2, 0), jnp.moveaxis(m_loc, 2, 0),
          jnp.moveaxis(kv_loc, 2, 0), jnp.moveaxis(n_loc, 2, 0))
    _, (C_prev, n_prev, m_prev) = lax.scan(step, init, xs)
    C_prev = jnp.moveaxis(C_prev, 0, 2)
    n_prev = jnp.moveaxis(n_prev, 0, 2)
    m_prev = jnp.moveaxis(m_prev, 0, 2)

    causal = jnp.tril(jnp.ones((L, L), dtype=bool))
    d_mat = jnp.where(causal, b[..., :, None] - b[..., None, :] + li[..., None, :], -jnp.inf)
    m_inter = b + m_prev[..., None]
    m_j = jnp.maximum(m_inter, jnp.max(d_mat, -1))
    s_mat = jnp.einsum('bhcjd,bhcsd->bhcjs', qc, kc) * jnp.exp(d_mat - m_j[..., None])
    inter_w = jnp.exp(m_inter - m_j)
    num = jnp.einsum('bhcjs,bhcsd->bhcjd', s_mat, vc) \
        + inter_w[..., None] * jnp.einsum('bhcvk,bhcjk->bhcjv', C_prev, qc)
    den = jnp.sum(s_mat, -1) + inter_w * jnp.einsum('bhck,bhcjk->bhcj', n_prev, qc)
    h = num / jnp.maximum(jnp.abs(den), jnp.exp(-m_j))[..., None]
    return h.reshape(B, H, S, D).astype(q.dtype)


def causal_conv(u, w, b):
    S = u.shape[1]
    up = jnp.pad(u, ((0, 0), (CONV_K - 1, 0), (0, 0)))
    return sum(up[:, j:j + S] * w[j] for j in range(CONV_K)) + b


def hybrid_mixer(h, w_in, conv_w, conv_b, gate_b_i, gate_b_f, lambda_q1, lambda_k1, lambda_q2, lambda_k2,
                 da_norm_g, ml_norm_g, w_out, rel_bias, layer_idx):
    B, S, _ = h.shape
    proj = h @ w_in
    offs, o = [], 0
    for c in IN_GROUPS[:-1]:
        o += c
        offs.append(o)
    da_q, da_k, da_v, ml_q, ml_k, ml_v, ml_o, ml_i, ml_f = jnp.split(proj, offs, axis=-1)

    lambda_init = 0.8 - 0.6 * math.exp(-0.3 * layer_idx)
    lam = (jnp.exp(jnp.sum(lambda_q1 * lambda_k1)) - jnp.exp(jnp.sum(lambda_q2 * lambda_k2))
           + lambda_init).astype(jnp.float32)
    qh = jnp.transpose(da_q.reshape(B, S, DA_HEADS, 2, DA_QK_DIM), (3, 0, 2, 1, 4))
    kh = jnp.transpose(da_k.reshape(B, S, DA_HEADS, 2, DA_QK_DIM), (3, 0, 2, 1, 4))
    vh = jnp.transpose(da_v.reshape(B, S, DA_HEADS, DA_V_DIM), (0, 2, 1, 3))
    da_out = diff_attention(qh[0], qh[1], kh[0], kh[1], vh, lam, rel_bias)
    da_out = rms_norm(da_out, da_norm_g) * (1.0 - lambda_init)
    da_out = jnp.transpose(da_out, (0, 2, 1, 3)).reshape(B, S, DA_WIDTH)

    qk = jax.nn.silu(causal_conv(jnp.concatenate([ml_q, ml_k], -1), conv_w, conv_b))
    mq, mk = jnp.split(qk, 2, axis=-1)
    to_heads = lambda t: jnp.transpose(t.reshape(B, S, ML_HEADS, ML_HEAD_DIM), (0, 2, 1, 3))
    i_pre = jnp.transpose(ml_i + gate_b_i, (0, 2, 1))
    f_pre = jnp.transpose(ml_f + gate_b_f, (0, 2, 1))
    ml_h = mlstm_chunkwise(to_heads(mq), to_heads(mk), to_heads(ml_v), i_pre, f_pre)
    ml_h = head_layer_norm(ml_h, ml_norm_g)
    ml_out = jnp.transpose(ml_h, (0, 2, 1, 3)).reshape(B, S, ML_WIDTH) * jax.nn.sigmoid(ml_o)

    return jnp.concatenate([da_out, ml_out], axis=-1) @ w_out


def setup_inputs(seed: int = 0) -> dict:
    key = jax.random.key(seed)
    ks = jax.random.split(key, 32)
    nrm = lambda k, shape, s: jax.random.normal(k, shape, jnp.float32) * s
    col_scale = np.ones((IN_COLS,), np.float32)
    v0 = IN_GROUPS[0] + IN_GROUPS[1]
    col_scale[v0:v0 + DA_WIDTH] = BETA
    mv0 = v0 + DA_WIDTH + 2 * ML_WIDTH
    col_scale[mv0:mv0 + ML_WIDTH] = BETA
    f_bias = jnp.linspace(3.0, 6.0, ML_HEADS, dtype=jnp.float32)
    return {
        "x": nrm(ks[0], (BATCH, SEQ, D_MODEL), 1.0),
        "ln1_g": 1.0 + nrm(ks[1], (DEPTH, D_MODEL), 0.02),
        "ln1_b": nrm(ks[2], (DEPTH, D_MODEL), 0.02),
        "ffn1_w_up": nrm(ks[3], (DEPTH, D_MODEL, 2 * D_FF), BETA * D_MODEL ** -0.5),
        "ffn1_w_down": nrm(ks[4], (DEPTH, D_FF, D_MODEL), BETA * D_FF ** -0.5),
        "w_in": nrm(ks[5], (DEPTH, D_MODEL, IN_COLS), D_MODEL ** -0.5) * jnp.asarray(col_scale),
        "conv_w": nrm(ks[6], (DEPTH, CONV_K, 2 * ML_WIDTH), CONV_K ** -0.5),
        "conv_b": nrm(ks[7], (DEPTH, 2 * ML_WIDTH), 0.02),
        "gate_b_i": nrm(ks[8], (DEPTH, ML_HEADS), 0.1),
        "gate_b_f": f_bias + nrm(ks[9], (DEPTH, ML_HEADS), 0.1),
        "lambda_q1": nrm(ks[10], (DEPTH, DA_QK_DIM), 0.1),
        "lambda_k1": nrm(ks[11], (DEPTH, DA_QK_DIM), 0.1),
        "lambda_q2": nrm(ks[12], (DEPTH, DA_QK_DIM), 0.1),
        "lambda_k2": nrm(ks[13], (DEPTH, DA_QK_DIM), 0.1),
        "da_norm_g": 1.0 + nrm(ks[14], (DEPTH, DA_V_DIM), 0.02),
        "ml_norm_g": 1.0 + nrm(ks[15], (DEPTH, ML_HEAD_DIM), 0.02),
        "w_out": nrm(ks[16], (DEPTH, MIX_WIDTH, D_MODEL), BETA * MIX_WIDTH ** -0.5),
        "ln2_g": 1.0 + nrm(ks[17], (DEPTH, D_MODEL), 0.02),
        "ln2_b": nrm(ks[18], (DEPTH, D_MODEL), 0.02),
        "ffn2_w_up": nrm(ks[19], (DEPTH, D_MODEL, 2 * D_FF), BETA * D_MODEL ** -0.5),
        "ffn2_w_down": nrm(ks[20], (DEPTH, D_FF, D_MODEL), BETA * D_FF ** -0.5),
        "ln3_g": 1.0 + nrm(ks[21], (DEPTH, D_MODEL), 0.02),
        "ln3_b": nrm(ks[22], (DEPTH, D_MODEL), 0.02),
        "rel_bias": nrm(ks[23], (N_BUCKETS, DA_HEADS), 0.2),
    }


def reference(x, ln1_g, ln1_b, ffn1_w_up, ffn1_w_down, w_in, conv_w, conv_b, gate_b_i, gate_b_f,
              lambda_q1, lambda_k1, lambda_q2, lambda_k2, da_norm_g, ml_norm_g, w_out,
              ln2_g, ln2_b, ffn2_w_up, ffn2_w_down, ln3_g, ln3_b, rel_bias):
    h = x
    for l in range(DEPTH):
        h = layer_norm(ALPHA * h + 0.5 * swiglu(h, ffn1_w_up[l], ffn1_w_down[l]), ln1_g[l], ln1_b[l])
        mix = hybrid_mixer(h, w_in[l], conv_w[l], conv_b[l], gate_b_i[l], gate_b_f[l],
                           lambda_q1[l], lambda_k1[l], lambda_q2[l], lambda_k2[l],
                           da_norm_g[l], ml_norm_g[l], w_out[l], rel_bias, l)
        h = layer_norm(ALPHA * h + mix, ln2_g[l], ln2_b[l])
        h = layer_norm(ALPHA * h + 0.5 * swiglu(h, ffn2_w_up[l], ffn2_w_down[l]), ln3_g[l], ln3_b[l])
    return h
```

```python
import functools
import math

import jax
import jax.numpy as jnp
from jax import lax
from jax.experimental import pallas as pl
from jax.experimental.pallas import tpu as pltpu

F32 = jnp.float32
BF16 = jnp.bfloat16

D_MODEL = 1024
D_FF = 2816
DEPTH = 1
CHUNK = 64
DA_HEADS = 4
DA_QK_DIM = 64
DA_V_DIM = 128
DA_WIDTH = DA_HEADS * DA_V_DIM
ML_HEADS = 4
ML_HEAD_DIM = 128
ML_WIDTH = ML_HEADS * ML_HEAD_DIM
CONV_K = 4
N_BUCKETS = 32
MAX_DISTANCE = 128
LN_EPS = 1e-5
ALPHA = (2.0 * DEPTH) ** 0.25
LAMBDA_INIT = 0.8 - 0.6 * math.exp(-0.3 * 0)

LANES = 128
SUBLANES = 8
MXU_DIM = 256
VMEM_LIMIT_BYTES = 56 * 1024 * 1024

ROW_TILE = 512
ATT_TILE = 512
ML_TILE = 256
FF_CHUNKS = ((0, 1024), (1024, 2048), (2048, D_FF))
MASKED = -1e30
GATE_LANES = LANES


def _const_spec(shape):
    zeros = (0,) * len(shape)
    return pl.BlockSpec(shape, lambda *_: zeros, pipeline_mode=pl.Buffered(1))


def _params(n_axes):
    return pltpu.CompilerParams(dimension_semantics=("arbitrary",) * n_axes,
                                vmem_limit_bytes=VMEM_LIMIT_BYTES)


def _layer_norm(y, g, b):
    mu = jnp.mean(y, -1, keepdims=True)
    d = y - mu
    var = jnp.mean(d * d, -1, keepdims=True)
    return d * lax.rsqrt(var + LN_EPS) * g + b


def _swiglu(xb, w_up_ref, w_down_ref):
    acc = None
    for lo, hi in FF_CHUNKS:
        a = jnp.dot(xb, w_up_ref[:, lo:hi], preferred_element_type=F32)
        u = jnp.dot(xb, w_up_ref[:, D_FF + lo:D_FF + hi], preferred_element_type=F32)
        act = (a * jax.nn.sigmoid(a) * u).astype(BF16)
        part = jnp.dot(act, w_down_ref[lo:hi, :], preferred_element_type=F32)
        acc = part if acc is None else acc + part
    return acc


def _ffn1_kernel(x_ref, wup_ref, wdn_ref, g_ref, b_ref, o_ref):
    x = x_ref[...]
    f = _swiglu(x.astype(BF16), wup_ref, wdn_ref)
    o_ref[...] = _layer_norm(ALPHA * x + 0.5 * f, g_ref[...], b_ref[...])


def _ffn1(x2, wup, wdn, g, b):
    n = x2.shape[0]
    row = pl.BlockSpec((ROW_TILE, D_MODEL), lambda i: (i, 0))
    return pl.pallas_call(
        _ffn1_kernel,
        grid=(n // ROW_TILE,),
        in_specs=[row, _const_spec(wup.shape), _const_spec(wdn.shape),
                  _const_spec(g.shape), _const_spec(b.shape)],
        out_specs=row,
        out_shape=jax.ShapeDtypeStruct((n, D_MODEL), F32),
        compiler_params=_params(1),
        name="ffn1",
    )(x2, wup, wdn, g, b)


def _in_proj_kernel(h_ref, wk_ref, wqt_ref, wvt_ref, wml_ref, wg_ref, wgt_ref,
                    k_ref, qt_ref, vt_ref, ml_ref, gc_ref, gt_ref):
    hb = h_ref[...].astype(BF16)
    nt = (((1,), (1,)), ((), ()))
    k_ref[...] = jnp.dot(hb, wk_ref[...], preferred_element_type=F32).astype(BF16)
    qt = lax.dot_general(wqt_ref[...], hb, nt, preferred_element_type=F32)
    qt_ref[0] = (qt * (DA_QK_DIM ** -0.5)).astype(BF16)
    vt_ref[0] = lax.dot_general(wvt_ref[...], hb, nt, preferred_element_type=F32).astype(BF16)
    ml_ref[...] = jnp.dot(hb, wml_ref[...], preferred_element_type=F32)
    gc_ref[...] = jnp.dot(hb, wg_ref[...], preferred_element_type=F32)
    gt_ref[...] = lax.dot_general(wgt_ref[...], hb, nt, preferred_element_type=F32)


def _in_proj(h1, wk, wqt, wvt, wml, wg, wgt):
    n = h1.shape[0]
    t = ATT_TILE
    nt = n // t
    return pl.pallas_call(
        _in_proj_kernel,
        grid=(nt,),
        in_specs=[pl.BlockSpec((t, D_MODEL), lambda i: (i, 0))]
        + [_const_spec(w.shape) for w in (wk, wqt, wvt, wml, wg, wgt)],
        out_specs=[
            pl.BlockSpec((t, DA_WIDTH), lambda i: (i, 0)),
            pl.BlockSpec((1, DA_WIDTH, t), lambda i: (i, 0, 0)),
            pl.BlockSpec((1, DA_WIDTH, t), lambda i: (i, 0, 0)),
            pl.BlockSpec((t, 4 * ML_WIDTH), lambda i: (i, 0)),
            pl.BlockSpec((t, GATE_LANES), lambda i: (i, 0)),
            pl.BlockSpec((SUBLANES, t), lambda i: (0, i)),
        ],
        out_shape=[
            jax.ShapeDtypeStruct((n, DA_WIDTH), BF16),
            jax.ShapeDtypeStruct((nt, DA_WIDTH, t), BF16),
            jax.ShapeDtypeStruct((nt, DA_WIDTH, t), BF16),
            jax.ShapeDtypeStruct((n, 4 * ML_WIDTH), F32),
            jax.ShapeDtypeStruct((n, GATE_LANES), F32),
            jax.ShapeDtypeStruct((SUBLANES, n), F32),
        ],
        compiler_params=_params(1),
        name="in_proj",
    )(h1, wk, wqt, wvt, wml, wg, wgt)


def _attn_kernel(q_ref, k_ref, v_ref, bd_ref, bs_ref, lq1_ref, lk1_ref, lq2_ref, lk2_ref, g_ref,
                 o_ref, q2_ref, m_ref, l_ref, acc_ref):
    t = ATT_TILE
    i = pl.program_id(2)

    qt = q_ref[0]
    row = lax.broadcasted_iota(jnp.int32, qt.shape, 0)
    zero = jnp.zeros_like(qt)
    q2_ref[:, :t] = jnp.where(row < DA_QK_DIM, qt, zero)
    q2_ref[:, t:] = jnp.where(row >= DA_QK_DIM, qt, zero)
    m_ref[...] = jnp.full(m_ref.shape, MASKED, F32)
    l_ref[...] = jnp.zeros(l_ref.shape, F32)
    acc_ref[...] = jnp.zeros(acc_ref.shape, F32)

    def step(j, bias):
        k_j = k_ref[pl.ds(pl.multiple_of(j * t, t), t), :]
        s = jnp.dot(k_j, q2_ref[...], preferred_element_type=F32)
        if bias is not None:
            s = s + jnp.concatenate([bias, bias], axis=1)
        m_old = m_ref[...]
        m_new = jnp.maximum(m_old, jnp.max(s, axis=0, keepdims=True))
        alpha = jnp.exp(m_old - m_new)
        p = jnp.exp(s - m_new)
        l_ref[...] = alpha * l_ref[...] + jnp.sum(p, axis=0, keepdims=True)
        pv = jnp.dot(v_ref[j], p.astype(BF16), preferred_element_type=F32)
        acc_ref[...] = alpha * acc_ref[...] + pv
        m_ref[...] = m_new

    def far_step(j, carry):
        step(j, None)
        return carry

    lax.fori_loop(0, jnp.maximum(i - 1, 0), far_step, 0)

    @pl.when(i >= 1)
    def _():
        step(i - 1, bs_ref[0])

    step(i, bd_ref[0])

    lam = (jnp.exp(jnp.sum(lq1_ref[...] * lk1_ref[...], keepdims=True))
           - jnp.exp(jnp.sum(lq2_ref[...] * lk2_ref[...], keepdims=True)) + LAMBDA_INIT)
    o = acc_ref[...] * (1.0 / l_ref[...])
    a = o[:, :t] - lam * o[:, t:]
    ms = jnp.mean(a * a, axis=0, keepdims=True)
    y = a * lax.rsqrt(ms + LN_EPS) * g_ref[...] * (1.0 - LAMBDA_INIT)
    o_ref[...] = y.T.astype(BF16)


def _attention(kr, qt, vt, bias_diag, bias_sub, lq1, lk1, lq2, lk2, g_col, batch, seq):
    t = ATT_TILE
    nq = seq // t
    lam_spec = _const_spec(lq1.shape)
    return pl.pallas_call(
        _attn_kernel,
        grid=(batch, DA_HEADS, nq),
        in_specs=[
            pl.BlockSpec((1, DA_V_DIM, t), lambda b, h, i: (b * nq + i, h, 0)),
            pl.BlockSpec((seq, DA_V_DIM), lambda b, h, i: (b, h)),
            pl.BlockSpec((nq, DA_V_DIM, t), lambda b, h, i: (b, h, 0)),
            pl.BlockSpec((1, t, t), lambda b, h, i: (h, 0, 0)),
            pl.BlockSpec((1, t, t), lambda b, h, i: (h, 0, 0)),
            lam_spec, lam_spec, lam_spec, lam_spec,
            _const_spec(g_col.shape),
        ],
        out_specs=pl.BlockSpec((t, DA_V_DIM), lambda b, h, i: (b * nq + i, h)),
        out_shape=jax.ShapeDtypeStruct((batch * seq, DA_WIDTH), BF16),
        scratch_shapes=[
            pltpu.VMEM((DA_V_DIM, 2 * t), BF16),
            pltpu.VMEM((1, 2 * t), F32),
            pltpu.VMEM((1, 2 * t), F32),
            pltpu.VMEM((DA_V_DIM, 2 * t), F32),
        ],
        compiler_params=_params(3),
        name="attn",
    )(qt, kr, vt, bias_diag, bias_sub, lq1, lk1, lq2, lk2, g_col)


def _log_sigmoid(x):
    return jnp.minimum(x, 0.0) - jnp.log1p(jnp.exp(-jnp.abs(x)))


def _mlstm_kernel(ml_ref, halo_ref, gc_ref, gt_ref, cw_ref, cb_ref, gbr_ref, gbc_ref, ng_ref,
                  o_ref, ext_ref, state_ref, m_ref):
    lc = ML_TILE
    hd = ML_HEAD_DIM
    c = pl.program_id(1)

    @pl.when(c == 0)
    def _():
        state_ref[...] = jnp.zeros(state_ref.shape, F32)
        m_ref[...] = jnp.zeros(m_ref.shape, F32)

    halo = halo_ref[...]
    ext_ref[0:SUBLANES, :] = jnp.where(c == 0, jnp.zeros_like(halo), halo)
    ext_ref[SUBLANES:, :] = ml_ref[:, 0:2 * ML_WIDTH]
    conv = cb_ref[...]
    for tap in range(CONV_K):
        off = SUBLANES - (CONV_K - 1) + tap
        conv = conv + ext_ref[pl.ds(off, lc), :] * cw_ref[tap:tap + 1, :]
    qk = conv * jax.nn.sigmoid(conv)

    gcol = gc_ref[...] + gbr_ref[...]
    grow = gt_ref[...] + gbc_ref[...]
    jj = lax.broadcasted_iota(jnp.int32, (lc, lc), 0)
    ss = lax.broadcasted_iota(jnp.int32, (lc, lc), 1)
    causal = ss <= jj
    tri = causal.astype(F32)
    tri_t = (jj <= ss).astype(F32)
    b_cols = jnp.dot(tri, _log_sigmoid(gcol), preferred_element_type=F32,
                     precision=lax.Precision.HIGHEST)
    b_rows = jnp.dot(_log_sigmoid(grow), tri_t, preferred_element_type=F32,
                     precision=lax.Precision.HIGHEST)

    lane = lax.broadcasted_iota(jnp.int32, (lc, hd), 1)
    ones_col = (lane == 0).astype(F32)

    for h in range(ML_HEADS):
        q_h = qk[:, h * hd:(h + 1) * hd].astype(BF16)
        k_h = (qk[:, ML_WIDTH + h * hd:ML_WIDTH + (h + 1) * hd] * (hd ** -0.5)).astype(BF16)
        v_h = ml_ref[:, 2 * ML_WIDTH + h * hd:2 * ML_WIDTH + (h + 1) * hd]
        v_aug = jnp.concatenate([v_h, ones_col], axis=1)

        b_c = b_cols[:, ML_HEADS + h:ML_HEADS + h + 1]
        li_c = gcol[:, h:h + 1]
        b_r = b_rows[ML_HEADS + h:ML_HEADS + h + 1, :]
        li_r = grow[h:h + 1, :]
        b_last = b_c[lc - 1:lc, :]
        m_prev = m_ref[h, 0:1, 0:1]

        g_c = b_last - b_c + li_c
        m_loc = jnp.max(g_c, axis=0, keepdims=True)
        w_c = jnp.exp(g_c - m_loc)
        m_new = jnp.maximum(b_last + m_prev, m_loc)
        decay = jnp.exp(b_last + m_prev - m_new)
        sc = jnp.exp(m_loc - m_new)

        d = jnp.where(causal, b_c - b_r + li_r, MASKED)
        m_inter = b_c + m_prev
        m_j = jnp.maximum(m_inter, jnp.max(d, axis=1, keepdims=True))
        qkt = lax.dot_general(q_h, k_h, (((1,), (1,)), ((), ())), preferred_element_type=F32)
        s_mat = qkt * jnp.exp(d - m_j)
        inter_w = jnp.exp(m_inter - m_j)
        state = state_ref[h]
        res = (jnp.dot(s_mat.astype(BF16), v_aug.astype(BF16), preferred_element_type=F32)
               + inter_w * jnp.dot(q_h, state.astype(BF16), preferred_element_type=F32))
        num = res[:, :hd]
        den = res[:, hd:hd + 1]
        hh = num / jnp.maximum(jnp.abs(den), jnp.exp(-m_j))

        kv = lax.dot_general(k_h, (w_c * v_aug).astype(BF16), (((0,), (0,)), ((), ())),
                             preferred_element_type=F32)
        state_ref[h] = decay * state + sc * kv
        m_ref[h] = jnp.broadcast_to(m_new, m_ref.shape[1:])

        mu = jnp.mean(hh, -1, keepdims=True)
        dv = hh - mu
        var = jnp.mean(dv * dv, -1, keepdims=True)
        normed = dv * lax.rsqrt(var + LN_EPS) * ng_ref[...]
        gate = jax.nn.sigmoid(ml_ref[:, 3 * ML_WIDTH + h * hd:3 * ML_WIDTH + (h + 1) * hd])
        o_ref[:, h * hd:(h + 1) * hd] = (normed * gate).astype(BF16)


def _mlstm(ml, gc, gt, conv_w, conv_b, gb_row, gb_col, ng, batch, seq):
    lc = ML_TILE
    nc = seq // lc
    halo_blocks = lc // SUBLANES
    return pl.pallas_call(
        _mlstm_kernel,
        grid=(batch, nc),
        in_specs=[
            pl.BlockSpec((lc, 4 * ML_WIDTH), lambda b, c: (b * nc + c, 0)),
            pl.BlockSpec((SUBLANES, 2 * ML_WIDTH),
                         lambda b, c: (jnp.maximum((b * nc + c) * halo_blocks - 1, 0), 0)),
            pl.BlockSpec((lc, GATE_LANES), lambda b, c: (b * nc + c, 0)),
            pl.BlockSpec((SUBLANES, lc), lambda b, c: (0, b * nc + c)),
            _const_spec(conv_w.shape), _const_spec(conv_b.shape),
            _const_spec(gb_row.shape), _const_spec(gb_col.shape), _const_spec(ng.shape),
        ],
        out_specs=pl.BlockSpec((lc, ML_WIDTH), lambda b, c: (b * nc + c, 0)),
        out_shape=jax.ShapeDtypeStruct((batch * seq, ML_WIDTH), BF16),
        scratch_shapes=[
            pltpu.VMEM((lc + SUBLANES, 2 * ML_WIDTH), F32),
            pltpu.VMEM((ML_HEADS, ML_HEAD_DIM, 2 * ML_HEAD_DIM), F32),
            pltpu.VMEM((ML_HEADS, SUBLANES, LANES), F32),
        ],
        compiler_params=_params(2),
        name="mlstm",
    )(ml, ml, gc, gt, conv_w, conv_b, gb_row, gb_col, ng)


def _out_ffn2_kernel(da_ref, mo_ref, h_ref, wo_ref, g2_ref, b2_ref, wup_ref, wdn_ref, g3_ref, b3_ref,
                     o_ref):
    mix = (jnp.dot(da_ref[...], wo_ref[0:DA_WIDTH, :], preferred_element_type=F32)
           + jnp.dot(mo_ref[...], wo_ref[DA_WIDTH:, :], preferred_element_type=F32))
    h2 = _layer_norm(ALPHA * h_ref[...] + mix, g2_ref[...], b2_ref[...])
    f = _swiglu(h2.astype(BF16), wup_ref, wdn_ref)
    o_ref[...] = _layer_norm(ALPHA * h2 + 0.5 * f, g3_ref[...], b3_ref[...])


def _out_ffn2(da, mo, h1, wo, g2, b2, wup, wdn, g3, b3):
    n = h1.shape[0]
    row = pl.BlockSpec((ROW_TILE, D_MODEL), lambda i: (i, 0))
    half = pl.BlockSpec((ROW_TILE, DA_WIDTH), lambda i: (i, 0))
    return pl.pallas_call(
        _out_ffn2_kernel,
        grid=(n // ROW_TILE,),
        in_specs=[half, half, row] + [_const_spec(a.shape) for a in (wo, g2, b2, wup, wdn, g3, b3)],
        out_specs=row,
        out_shape=jax.ShapeDtypeStruct((n, D_MODEL), F32),
        compiler_params=_params(1),
        name="out_ffn2",
    )(da, mo, h1, wo, g2, b2, wup, wdn, g3, b3)


def _t5_bucket(rel):
    nb = N_BUCKETS // 2
    max_exact = nb // 2
    ret = jnp.where(rel > 0, nb, 0)
    n = jnp.abs(rel)
    nf = jnp.maximum(n, max_exact).astype(jnp.float32)
    large = max_exact + (jnp.log(nf / max_exact) / math.log(MAX_DISTANCE / max_exact)
                         * (nb - max_exact)).astype(jnp.int32)
    large = jnp.minimum(large, nb - 1)
    return ret + jnp.where(n < max_exact, n, large)


def _bias_tiles(rel_bias, seq):
    t = ATT_TILE
    kpos = jnp.arange(t, dtype=jnp.int32)[:, None]
    qpos = jnp.arange(t, dtype=jnp.int32)[None, :]
    far = rel_bias[_t5_bucket(jnp.int32(-(seq - 1)))]
    diag = jnp.transpose(rel_bias[_t5_bucket(kpos - qpos)] - far, (2, 0, 1))
    allowed = (kpos // CHUNK) <= (qpos // CHUNK)
    diag = jnp.where(allowed[None], diag, MASKED)
    sub = jnp.transpose(rel_bias[_t5_bucket(kpos - t - qpos)] - far, (2, 0, 1))
    return diag.astype(F32), sub.astype(F32)


def kernel(x, ln1_g, ln1_b, ffn1_w_up, ffn1_w_down, w_in, conv_w, conv_b, gate_b_i, gate_b_f,
           lambda_q1, lambda_k1, lambda_q2, lambda_k2, da_norm_g, ml_norm_g, w_out,
           ln2_g, ln2_b, ffn2_w_up, ffn2_w_down, ln3_g, ln3_b, rel_bias):
    batch, seq, _ = x.shape
    assert seq % ATT_TILE == 0 and seq % ML_TILE == 0 and seq >= 2 * ATT_TILE
    assert (batch * seq) % ROW_TILE == 0
    n = batch * seq
    l = 0
    row = lambda v: v.reshape(1, -1).astype(F32)

    w = w_in[l]
    o = 0
    wq = w[:, o:o + DA_WIDTH]; o += DA_WIDTH
    wk = w[:, o:o + DA_WIDTH]; o += DA_WIDTH
    wv = w[:, o:o + DA_WIDTH]; o += DA_WIDTH
    wml = w[:, o:o + 4 * ML_WIDTH]; o += 4 * ML_WIDTH
    wgate = w[:, o:o + 2 * ML_HEADS]
    wg = jnp.pad(wgate, ((0, 0), (0, GATE_LANES - 2 * ML_HEADS))).astype(BF16)
    gb = jnp.concatenate([gate_b_i[l], gate_b_f[l]]).astype(F32)
    gb_row = jnp.pad(gb, (0, GATE_LANES - 2 * ML_HEADS)).reshape(1, GATE_LANES)
    gb_col = gb.reshape(2 * ML_HEADS, 1)

    h1 = _ffn1(x.reshape(n, D_MODEL), ffn1_w_up[l].astype(BF16), ffn1_w_down[l].astype(BF16),
               row(ln1_g[l]), row(ln1_b[l]))

    kr, qt, vt, ml, gc, gt = _in_proj(
        h1, wk.astype(BF16), wq.T.astype(BF16), wv.T.astype(BF16), wml.astype(BF16),
        wg, wgate.T.astype(BF16))

    bias_diag, bias_sub = _bias_tiles(rel_bias.astype(F32), seq)
    da = _attention(kr, qt, vt, bias_diag, bias_sub,
                    row(lambda_q1[l]), row(lambda_k1[l]), row(lambda_q2[l]), row(lambda_k2[l]),
                    da_norm_g[l].reshape(DA_V_DIM, 1).astype(F32), batch, seq)

    mo = _mlstm(ml, gc, gt, conv_w[l].astype(F32), row(conv_b[l]), gb_row, gb_col,
                row(ml_norm_g[l]), batch, seq)

    out = _out_ffn2(da, mo, h1, w_out[l].astype(BF16), row(ln2_g[l]), row(ln2_b[l]),
                    ffn2_w_up[l].astype(BF16), ffn2_w_down[l].astype(BF16),
                    row(ln3_g[l]), row(ln3_b[l]))
    return out.reshape(batch, seq, D_MODEL)
```

```python
import functools
import math

import jax
import jax.numpy as jnp
from jax import lax
from jax.experimental import pallas as pl
from jax.experimental.pallas import tpu as pltpu

F32 = jnp.float32
BF16 = jnp.bfloat16

D_MODEL = 1024
D_FF = 2816
DEPTH = 1
CHUNK = 64
DA_HEADS = 4
DA_QK_DIM = 64
DA_V_DIM = 128
DA_WIDTH = DA_HEADS * DA_V_DIM
ML_HEADS = 4
ML_HEAD_DIM = 128
ML_WIDTH = ML_HEADS * ML_HEAD_DIM
CONV_K = 4
N_BUCKETS = 32
MAX_DISTANCE = 128
LN_EPS = 1e-5
ALPHA = (2.0 * DEPTH) ** 0.25
LAMBDA_INIT = 0.8 - 0.6 * math.exp(-0.3 * 0)

LANES = 128
SUBLANES = 8
MXU_DIM = 256
VMEM_LIMIT_BYTES = 56 * 1024 * 1024

ROW_TILE = 512
ATT_TILE = 512
ML_TILE = 256
FF_CHUNKS = ((0, 1024), (1024, 2048), (2048, D_FF))
MASKED = -1e30
LOG2E = math.log2(math.e)
Q_SCALE = DA_QK_DIM ** -0.5 * LOG2E
V_ROWS = DA_V_DIM + 16
GATE_LANES = LANES


def _const_spec(shape):
    zeros = (0,) * len(shape)
    return pl.BlockSpec(shape, lambda *_: zeros, pipeline_mode=pl.Buffered(1))


def _params(n_axes):
    return pltpu.CompilerParams(dimension_semantics=("arbitrary",) * n_axes,
                                vmem_limit_bytes=VMEM_LIMIT_BYTES)


def _layer_norm(y, g, b):
    mu = jnp.mean(y, -1, keepdims=True)
    d = y - mu
    var = jnp.mean(d * d, -1, keepdims=True)
    return d * lax.rsqrt(var + LN_EPS) * g + b


def _swiglu(xb, w_up_ref, w_down_ref):
    acc = None
    for lo, hi in FF_CHUNKS:
        a = jnp.dot(xb, w_up_ref[:, lo:hi], preferred_element_type=F32)
        u = jnp.dot(xb, w_up_ref[:, D_FF + lo:D_FF + hi], preferred_element_type=F32)
        act = (a * jax.nn.sigmoid(a) * u).astype(BF16)
        part = jnp.dot(act, w_down_ref[lo:hi, :], preferred_element_type=F32)
        acc = part if acc is None else acc + part
    return acc


def _ffn1_kernel(x_ref, wup_ref, wdn_ref, g_ref, b_ref, o_ref):
    x = x_ref[...]
    f = _swiglu(x.astype(BF16), wup_ref, wdn_ref)
    o_ref[...] = _layer_norm(ALPHA * x + 0.5 * f, g_ref[...], b_ref[...])


def _ffn1(x2, wup, wdn, g, b):
    n = x2.shape[0]
    row = pl.BlockSpec((ROW_TILE, D_MODEL), lambda i: (i, 0))
    return pl.pallas_call(
        _ffn1_kernel,
        grid=(n // ROW_TILE,),
        in_specs=[row, _const_spec(wup.shape), _const_spec(wdn.shape),
                  _const_spec(g.shape), _const_spec(b.shape)],
        out_specs=row,
        out_shape=jax.ShapeDtypeStruct((n, D_MODEL), F32),
        compiler_params=_params(1),
        name="ffn1",
    )(x2, wup, wdn, g, b)


def _in_proj_kernel(h_ref, wk_ref, wqt_ref, wvt_ref, wml_ref, wg_ref, wgt_ref,
                    k_ref, qt_ref, vt_ref, ml_ref, gc_ref, gt_ref):
    hb = h_ref[...].astype(BF16)
    nt = (((1,), (1,)), ((), ()))
    k_ref[...] = jnp.dot(hb, wk_ref[...], preferred_element_type=F32).astype(BF16)
    qt = lax.dot_general(wqt_ref[...], hb, nt, preferred_element_type=F32)
    qt_ref[0] = (qt * Q_SCALE).astype(BF16)
    vt = lax.dot_general(wvt_ref[...], hb, nt, preferred_element_type=F32).astype(BF16)
    pad_row = lax.broadcasted_iota(jnp.int32, (V_ROWS - DA_V_DIM, vt.shape[1]), 0)
    ones_pad = (pad_row == 0).astype(BF16)
    for h in range(DA_HEADS):
        vt_ref[0, h, 0:DA_V_DIM, :] = vt[h * DA_V_DIM:(h + 1) * DA_V_DIM]
        vt_ref[0, h, DA_V_DIM:, :] = ones_pad
    ml_ref[...] = jnp.dot(hb, wml_ref[...], preferred_element_type=F32)
    gc_ref[...] = jnp.dot(hb, wg_ref[...], preferred_element_type=F32)
    gt_ref[...] = lax.dot_general(wgt_ref[...], hb, nt, preferred_element_type=F32)


def _in_proj(h1, wk, wqt, wvt, wml, wg, wgt):
    n = h1.shape[0]
    t = ATT_TILE
    nt = n // t
    return pl.pallas_call(
        _in_proj_kernel,
        grid=(nt,),
        in_specs=[pl.BlockSpec((t, D_MODEL), lambda i: (i, 0))]
        + [_const_spec(w.shape) for w in (wk, wqt, wvt, wml, wg, wgt)],
        out_specs=[
            pl.BlockSpec((t, DA_WIDTH), lambda i: (i, 0)),
            pl.BlockSpec((1, DA_WIDTH, t), lambda i: (i, 0, 0)),
            pl.BlockSpec((1, DA_HEADS, V_ROWS, t), lambda i: (i, 0, 0, 0)),
            pl.BlockSpec((t, 4 * ML_WIDTH), lambda i: (i, 0)),
            pl.BlockSpec((t, GATE_LANES), lambda i: (i, 0)),
            pl.BlockSpec((SUBLANES, t), lambda i: (0, i)),
        ],
        out_shape=[
            jax.ShapeDtypeStruct((n, DA_WIDTH), BF16),
            jax.ShapeDtypeStruct((nt, DA_WIDTH, t), BF16),
            jax.ShapeDtypeStruct((nt, DA_HEADS, V_ROWS, t), BF16),
            jax.ShapeDtypeStruct((n, 4 * ML_WIDTH), F32),
            jax.ShapeDtypeStruct((n, GATE_LANES), F32),
            jax.ShapeDtypeStruct((SUBLANES, n), F32),
        ],
        compiler_params=_params(1),
        name="in_proj",
    )(h1, wk, wqt, wvt, wml, wg, wgt)


def _attn_kernel(q_ref, k_ref, v_ref, bd_ref, bs_ref, lq1_ref, lk1_ref, lq2_ref, lk2_ref, g_ref,
                 o_ref, q2_ref, m_ref, acc_ref, s0_ref, s1_ref):
    t = ATT_TILE
    i = pl.program_id(2)

    qt = q_ref[0]
    row = lax.broadcasted_iota(jnp.int32, qt.shape, 0)
    zero = jnp.zeros_like(qt)
    q2_ref[:, :t] = jnp.where(row < DA_QK_DIM, qt, zero)
    q2_ref[:, t:] = jnp.where(row >= DA_QK_DIM, qt, zero)
    m_ref[...] = jnp.full(m_ref.shape, MASKED, F32)
    acc_ref[...] = jnp.zeros(acc_ref.shape, F32)

    def logits(j, s_ref):
        k_j = k_ref[pl.ds(pl.multiple_of(j * t, t), t), :]
        s_ref[...] = jnp.dot(k_j, q2_ref[...], preferred_element_type=F32)

    def softmax_pv(j, s_ref, bias):
        v_j = v_ref[j, 0]
        for half in range(2):
            cols = slice(half * t, (half + 1) * t)
            s = s_ref[:, cols]
            if bias is not None:
                s = s + bias
            m_old = m_ref[:, cols]
            m_new = jnp.maximum(m_old, jnp.max(s, axis=0, keepdims=True))
            alpha = jnp.exp2(m_old - m_new)
            p = jnp.exp2(s - m_new).astype(BF16)
            pv = jnp.dot(v_j, p, preferred_element_type=F32)
            acc_ref[:, cols] = alpha * acc_ref[:, cols] + pv
            m_ref[:, cols] = m_new

    def pipe(j, cur_ref, nxt_ref, bias):
        logits(j + 1, nxt_ref)
        softmax_pv(j, cur_ref, bias)

    def far_pair(jj, carry):
        pipe(2 * jj, s0_ref, s1_ref, None)
        pipe(2 * jj + 1, s1_ref, s0_ref, None)
        return carry

    n_far = jnp.maximum(i - 1, 0)
    logits(0, s0_ref)
    lax.fori_loop(0, n_far // 2, far_pair, 0)

    @pl.when(i == 0)
    def _():
        softmax_pv(i, s0_ref, bd_ref[0])

    @pl.when(i % 2 == 1)
    def _():
        pipe(i - 1, s0_ref, s1_ref, bs_ref[0])
        softmax_pv(i, s1_ref, bd_ref[0])

    @pl.when(jnp.logical_and(i % 2 == 0, i >= 2))
    def _():
        pipe(i - 2, s0_ref, s1_ref, None)
        pipe(i - 1, s1_ref, s0_ref, bs_ref[0])
        softmax_pv(i, s0_ref, bd_ref[0])

    lam = (jnp.exp(jnp.sum(lq1_ref[...] * lk1_ref[...], keepdims=True))
           - jnp.exp(jnp.sum(lq2_ref[...] * lk2_ref[...], keepdims=True)) + LAMBDA_INIT)
    o = acc_ref[0:DA_V_DIM, :] * (1.0 / acc_ref[DA_V_DIM:DA_V_DIM + 1, :])
    a = o[:, :t] - lam * o[:, t:]
    ms = jnp.mean(a * a, axis=0, keepdims=True)
    y = a * lax.rsqrt(ms + LN_EPS) * g_ref[...] * (1.0 - LAMBDA_INIT)
    o_ref[...] = y.T.astype(BF16)


def _attention(kr, qt, vt, bias_diag, bias_sub, lq1, lk1, lq2, lk2, g_col, batch, seq):
    t = ATT_TILE
    nq = seq // t
    lam_spec = _const_spec(lq1.shape)
    return pl.pallas_call(
        _attn_kernel,
        grid=(batch, DA_HEADS, nq),
        in_specs=[
            pl.BlockSpec((1, DA_V_DIM, t), lambda b, h, i: (b * nq + i, h, 0)),
            pl.BlockSpec((seq, DA_V_DIM), lambda b, h, i: (b, h)),
            pl.BlockSpec((nq, 1, V_ROWS, t), lambda b, h, i: (b, h, 0, 0)),
            pl.BlockSpec((1, t, t), lambda b, h, i: (h, 0, 0)),
            pl.BlockSpec((1, t, t), lambda b, h, i: (h, 0, 0)),
            lam_spec, lam_spec, lam_spec, lam_spec,
            _const_spec(g_col.shape),
        ],
        out_specs=pl.BlockSpec((t, DA_V_DIM), lambda b, h, i: (b * nq + i, h)),
        out_shape=jax.ShapeDtypeStruct((batch * seq, DA_WIDTH), BF16),
        scratch_shapes=[
            pltpu.VMEM((DA_V_DIM, 2 * t), BF16),
            pltpu.VMEM((1, 2 * t), F32),
            pltpu.VMEM((V_ROWS, 2 * t), F32),
            pltpu.VMEM((t, 2 * t), F32),
            pltpu.VMEM((t, 2 * t), F32),
        ],
        compiler_params=_params(3),
        name="attn",
    )(qt, kr, vt, bias_diag, bias_sub, lq1, lk1, lq2, lk2, g_col)


def _log_sigmoid(x):
    return jnp.minimum(x, 0.0) - jnp.log1p(jnp.exp(-jnp.abs(x)))


def _mlstm_kernel(ml_ref, halo_ref, gc_ref, gt_ref, cw_ref, cb_ref, gbr_ref, gbc_ref, ng_ref,
                  o_ref, ext_ref, state_ref, m_ref):
    lc = ML_TILE
    hd = ML_HEAD_DIM
    c = pl.program_id(1)

    @pl.when(c == 0)
    def _():
        state_ref[...] = jnp.zeros(state_ref.shape, F32)
        m_ref[...] = jnp.zeros(m_ref.shape, F32)

    halo = halo_ref[...]
    ext_ref[0:SUBLANES, :] = jnp.where(c == 0, jnp.zeros_like(halo), halo)
    ext_ref[SUBLANES:, :] = ml_ref[:, 0:2 * ML_WIDTH]
    conv = cb_ref[...]
    for tap in range(CONV_K):
        off = SUBLANES - (CONV_K - 1) + tap
        conv = conv + ext_ref[pl.ds(off, lc), :] * cw_ref[tap:tap + 1, :]
    qk = conv * jax.nn.sigmoid(conv)

    gcol = gc_ref[...] + gbr_ref[...]
    grow = gt_ref[...] + gbc_ref[...]
    jj = lax.broadcasted_iota(jnp.int32, (lc, lc), 0)
    ss = lax.broadcasted_iota(jnp.int32, (lc, lc), 1)
    causal = ss <= jj
    tri = causal.astype(F32)
    tri_t = (jj <= ss).astype(F32)
    b_cols = jnp.dot(tri, _log_sigmoid(gcol), preferred_element_type=F32,
                     precision=lax.Precision.HIGHEST)
    b_rows = jnp.dot(_log_sigmoid(grow), tri_t, preferred_element_type=F32,
                     precision=lax.Precision.HIGHEST)

    lane = lax.broadcasted_iota(jnp.int32, (lc, hd), 1)
    ones_col = (lane == 0).astype(F32)

    for h in range(ML_HEADS):
        q_h = qk[:, h * hd:(h + 1) * hd].astype(BF16)
        k_h = (qk[:, ML_WIDTH + h * hd:ML_WIDTH + (h + 1) * hd] * (hd ** -0.5)).astype(BF16)
        v_h = ml_ref[:, 2 * ML_WIDTH + h * hd:2 * ML_WIDTH + (h + 1) * hd]
        v_aug = jnp.concatenate([v_h, ones_col], axis=1)

        b_c = b_cols[:, ML_HEADS + h:ML_HEADS + h + 1]
        li_c = gcol[:, h:h + 1]
        b_r = b_rows[ML_HEADS + h:ML_HEADS + h + 1, :]
        li_r = grow[h:h + 1, :]
        b_last = b_c[lc - 1:lc, :]
        m_prev = m_ref[h, 0:1, 0:1]

        g_c = b_last - b_c + li_c
        m_loc = jnp.max(g_c, axis=0, keepdims=True)
        w_c = jnp.exp(g_c - m_loc)
        m_new = jnp.maximum(b_last + m_prev, m_loc)
        decay = jnp.exp(b_last + m_prev - m_new)
        sc = jnp.exp(m_loc - m_new)

        d = jnp.where(causal, b_c - b_r + li_r, MASKED)
        m_inter = b_c + m_prev
        m_j = jnp.maximum(m_inter, jnp.max(d, axis=1, keepdims=True))
        qkt = lax.dot_general(q_h, k_h, (((1,), (1,)), ((), ())), preferred_element_type=F32)
        s_mat = qkt * jnp.exp(d - m_j)
        inter_w = jnp.exp(m_inter - m_j)
        state = state_ref[h]
        res = (jnp.dot(s_mat.astype(BF16), v_aug.astype(BF16), preferred_element_type=F32)
               + inter_w * jnp.dot(q_h, state.astype(BF16), preferred_element_type=F32))
        num = res[:, :hd]
        den = res[:, hd:hd + 1]
        hh = num / jnp.maximum(jnp.abs(den), jnp.exp(-m_j))

        kv = lax.dot_general(k_h, (w_c * v_aug).astype(BF16), (((0,), (0,)), ((), ())),
                             preferred_element_type=F32)
        state_ref[h] = decay * state + sc * kv
        m_ref[h] = jnp.broadcast_to(m_new, m_ref.shape[1:])

        mu = jnp.mean(hh, -1, keepdims=True)
        dv = hh - mu
        var = jnp.mean(dv * dv, -1, keepdims=True)
        normed = dv * lax.rsqrt(var + LN_EPS) * ng_ref[...]
        gate = jax.nn.sigmoid(ml_ref[:, 3 * ML_WIDTH + h * hd:3 * ML_WIDTH + (h + 1) * hd])
        o_ref[:, h * hd:(h + 1) * hd] = (normed * gate).astype(BF16)


def _mlstm(ml, gc, gt, conv_w, conv_b, gb_row, gb_col, ng, batch, seq):
    lc = ML_TILE
    nc = seq // lc
    halo_blocks = lc // SUBLANES
    return pl.pallas_call(
        _mlstm_kernel,
        grid=(batch, nc),
        in_specs=[
            pl.BlockSpec((lc, 4 * ML_WIDTH), lambda b, c: (b * nc + c, 0)),
            pl.BlockSpec((SUBLANES, 2 * ML_WIDTH),
                         lambda b, c: (jnp.maximum((b * nc + c) * halo_blocks - 1, 0), 0)),
            pl.BlockSpec((lc, GATE_LANES), lambda b, c: (b * nc + c, 0)),
            pl.BlockSpec((SUBLANES, lc), lambda b, c: (0, b * nc + c)),
            _const_spec(conv_w.shape), _const_spec(conv_b.shape),
            _const_spec(gb_row.shape), _const_spec(gb_col.shape), _const_spec(ng.shape),
        ],
        out_specs=pl.BlockSpec((lc, ML_WIDTH), lambda b, c: (b * nc + c, 0)),
        out_shape=jax.ShapeDtypeStruct((batch * seq, ML_WIDTH), BF16),
        scratch_shapes=[
            pltpu.VMEM((lc + SUBLANES, 2 * ML_WIDTH), F32),
            pltpu.VMEM((ML_HEADS, ML_HEAD_DIM, 2 * ML_HEAD_DIM), F32),
            pltpu.VMEM((ML_HEADS, SUBLANES, LANES), F32),
        ],
        compiler_params=_params(2),
        name="mlstm",
    )(ml, ml, gc, gt, conv_w, conv_b, gb_row, gb_col, ng)


def _out_ffn2_kernel(da_ref, mo_ref, h_ref, wo_ref, g2_ref, b2_ref, wup_ref, wdn_ref, g3_ref, b3_ref,
                     o_ref):
    mix = (jnp.dot(da_ref[...], wo_ref[0:DA_WIDTH, :], preferred_element_type=F32)
           + jnp.dot(mo_ref[...], wo_ref[DA_WIDTH:, :], preferred_element_type=F32))
    h2 = _layer_norm(ALPHA * h_ref[...] + mix, g2_ref[...], b2_ref[...])
    f = _swiglu(h2.astype(BF16), wup_ref, wdn_ref)
    o_ref[...] = _layer_norm(ALPHA * h2 + 0.5 * f, g3_ref[...], b3_ref[...])


def _out_ffn2(da, mo, h1, wo, g2, b2, wup, wdn, g3, b3):
    n = h1.shape[0]
    row = pl.BlockSpec((ROW_TILE, D_MODEL), lambda i: (i, 0))
    half = pl.BlockSpec((ROW_TILE, DA_WIDTH), lambda i: (i, 0))
    return pl.pallas_call(
        _out_ffn2_kernel,
        grid=(n // ROW_TILE,),
        in_specs=[half, half, row] + [_const_spec(a.shape) for a in (wo, g2, b2, wup, wdn, g3, b3)],
        out_specs=row,
        out_shape=jax.ShapeDtypeStruct((n, D_MODEL), F32),
        compiler_params=_params(1),
        name="out_ffn2",
    )(da, mo, h1, wo, g2, b2, wup, wdn, g3, b3)


def _t5_bucket(rel):
    nb = N_BUCKETS // 2
    max_exact = nb // 2
    ret = jnp.where(rel > 0, nb, 0)
    n = jnp.abs(rel)
    nf = jnp.maximum(n, max_exact).astype(jnp.float32)
    large = max_exact + (jnp.log(nf / max_exact) / math.log(MAX_DISTANCE / max_exact)
                         * (nb - max_exact)).astype(jnp.int32)
    large = jnp.minimum(large, nb - 1)
    return ret + jnp.where(n < max_exact, n, large)


def _bias_tiles(rel_bias, seq):
    t = ATT_TILE
    kpos = jnp.arange(t, dtype=jnp.int32)[:, None]
    qpos = jnp.arange(t, dtype=jnp.int32)[None, :]
    def lookup(bucket):
        out = jnp.zeros((DA_HEADS,) + bucket.shape, F32)
        for b in range(N_BUCKETS):
            out = jnp.where(bucket[None] == b, rel_bias[b][:, None, None], out)
        return out

    far = rel_bias[_t5_bucket(jnp.int32(-(seq - 1)))][:, None, None]
    allowed = (kpos // CHUNK) <= (qpos // CHUNK)
    diag = jnp.where(allowed[None], (lookup(_t5_bucket(kpos - qpos)) - far) * LOG2E, MASKED)
    sub = (lookup(_t5_bucket(kpos - t - qpos)) - far) * LOG2E
    return diag, sub


def kernel(x, ln1_g, ln1_b, ffn1_w_up, ffn1_w_down, w_in, conv_w, conv_b, gate_b_i, gate_b_f,
           lambda_q1, lambda_k1, lambda_q2, lambda_k2, da_norm_g, ml_norm_g, w_out,
           ln2_g, ln2_b, ffn2_w_up, ffn2_w_down, ln3_g, ln3_b, rel_bias):
    batch, seq, _ = x.shape
    assert seq % ATT_TILE == 0 and seq % ML_TILE == 0 and seq >= 2 * ATT_TILE
    assert (batch * seq) % ROW_TILE == 0
    n = batch * seq
    l = 0
    row = lambda v: v.reshape(1, -1).astype(F32)

    w = w_in[l]
    o = 0
    wq = w[:, o:o + DA_WIDTH]; o += DA_WIDTH
    wk = w[:, o:o + DA_WIDTH]; o += DA_WIDTH
    wv = w[:, o:o + DA_WIDTH]; o += DA_WIDTH
    wml = w[:, o:o + 4 * ML_WIDTH]; o += 4 * ML_WIDTH
    wgate = w[:, o:o + 2 * ML_HEADS]
    wg = jnp.pad(wgate, ((0, 0), (0, GATE_LANES - 2 * ML_HEADS))).astype(BF16)
    gb = jnp.concatenate([gate_b_i[l], gate_b_f[l]]).astype(F32)
    gb_row = jnp.pad(gb, (0, GATE_LANES - 2 * ML_HEADS)).reshape(1, GATE_LANES)
    gb_col = gb.reshape(2 * ML_HEADS, 1)

    h1 = _ffn1(x.reshape(n, D_MODEL), ffn1_w_up[l].astype(BF16), ffn1_w_down[l].astype(BF16),
               row(ln1_g[l]), row(ln1_b[l]))

    kr, qt, vt, ml, gc, gt = _in_proj(
        h1, wk.astype(BF16), wq.T.astype(BF16), wv.T.astype(BF16), wml.astype(BF16),
        wg, wgate.T.astype(BF16))

    bias_diag, bias_sub = _bias_tiles(rel_bias.astype(F32), seq)
    da = _attention(kr, qt, vt, bias_diag, bias_sub,
                    row(lambda_q1[l]), row(lambda_k1[l]), row(lambda_q2[l]), row(lambda_k2[l]),
                    da_norm_g[l].reshape(DA_V_DIM, 1).astype(F32), batch, seq)

    mo = _mlstm(ml, gc, gt, conv_w[l].astype(F32), row(conv_b[l]), gb_row, gb_col,
                row(ml_norm_g[l]), batch, seq)

    out = _out_ffn2(da, mo, h1, w_out[l].astype(BF16), row(ln2_g[l]), row(ln2_b[l]),
                    ffn2_w_up[l].astype(BF16), ffn2_w_down[l].astype(BF16),
                    row(ln3_g[l]), row(ln3_b[l]))
    return out.reshape(batch, seq, D_MODEL)
```
